```python
import math
import jax, jax.numpy as jnp
from jax import lax
import numpy as np

D_MODEL = 1024
BATCH = 16
SEQ = 2048
DEPTH = 2

MIX_WIDTH = D_MODEL
M_DK = 64
M_DV = 64
M_WIDTH = MIX_WIDTH // 4
M_HEADS = M_WIDTH // M_DV
M_QK_WIDTH = M_HEADS * M_DK
M_CHUNK = 64
CONV_W = 4
A_DH = 64
A_WIDTH = MIX_WIDTH // 4
A_HEADS = A_WIDTH // A_DH
MOBA_BLOCK = 256
MOBA_TOPK = 3
MOBA_QCHUNK = 16
DF_DQK = 64
DF_DV = 2 * DF_DQK
DF_WIDTH = MIX_WIDTH // 2
DF_HEADS = DF_WIDTH // DF_DV
DF_QK_WIDTH = DF_HEADS * 2 * DF_DQK
Q_BLOCK = 128
N_SOFTMAX_HEADS = A_HEADS + DF_HEADS
D_FF = 2816
RMS_EPS = 1e-6
SPLIT_SIZES = (M_QK_WIDTH, M_QK_WIDTH, M_WIDTH, M_WIDTH, M_HEADS, M_HEADS,
               A_WIDTH, A_WIDTH, A_WIDTH,
               DF_QK_WIDTH, DF_QK_WIDTH, DF_WIDTH)
PROJ_WIDTH = sum(SPLIT_SIZES)

kernel_name = "hymba_style_mlstm_moba_diffattn_macaron"

F32 = jnp.float32


def rmsnorm(x, g):
    xf = x.astype(F32)
    y = xf * lax.rsqrt(jnp.mean(xf * xf, axis=-1, keepdims=True) + RMS_EPS) * g.astype(F32)
    return y.astype(x.dtype)


def swiglu(h, w_gate, w_up, w_down):
    return (jax.nn.silu(h @ w_gate) * (h @ w_up)) @ w_down


def split_cols(p):
    outs, o = [], 0
    for s in SPLIT_SIZES:
        outs.append(p[..., o:o + s])
        o += s
    return outs


def to_heads(t, n_heads):
    b, s, _ = t.shape
    return t.reshape(b, s, n_heads, -1).transpose(0, 2, 1, 3)


def from_heads(t):
    b, h, s, d = t.shape
    return t.transpose(0, 2, 1, 3).reshape(b, s, h * d)


def causal_conv(u, w):
    s = u.shape[1]
    up = jnp.pad(u, ((0, 0), (CONV_W - 1, 0), (0, 0)))
    return sum(up[:, j:j + s] * w[j] for j in range(CONV_W))


def alibi_slopes():
    n = N_SOFTMAX_HEADS
    return jnp.exp2(-8.0 * jnp.arange(1, n + 1, dtype=F32) / n)


def mlstm_chunkwise(q, k, v, i_pre, f_pre):
    b_, h_, s_, dk = q.shape
    dv = v.shape[-1]
    L = M_CHUNK
    nc = s_ // L
    qc = q.astype(F32).reshape(b_, h_, nc, L, dk)
    kc = k.astype(F32).reshape(b_, h_, nc, L, dk) * (dk ** -0.5)
    vc = v.astype(F32).reshape(b_, h_, nc, L, dv)
    log_f = jax.nn.log_sigmoid(f_pre.astype(F32)).reshape(b_, h_, nc, L)
    ig = i_pre.astype(F32).reshape(b_, h_, nc, L)
    bcum = jnp.cumsum(log_f, axis=-1)
    g = bcum[..., -1]
    a = g[..., None] - bcum + ig
    m_loc = jnp.max(a, axis=-1)
    wgt = jnp.exp(a - m_loc[..., None])
    c_loc = jnp.einsum('bhcl,bhcld,bhcle->bhcde', wgt, kc, vc)
    n_loc = jnp.einsum('bhcl,bhcld->bhcd', wgt, kc)

    def step(carry, xs):
        c_st, n_st, m_st = carry
        g_c, m_l, c_l, n_l = xs
        m_new = jnp.maximum(g_c + m_st, m_l)
        sp = jnp.exp(g_c + m_st - m_new)
        sl = jnp.exp(m_l - m_new)
        c_new = sp[..., None, None] * c_st + sl[..., None, None] * c_l
        n_new = sp[..., None] * n_st + sl[..., None] * n_l
        return (c_new, n_new, m_new), (c_st, n_st, m_st)

    init = (jnp.zeros((b_, h_, dk, dv), F32), jnp.zeros((b_, h_, dk), F32), jnp.zeros((b_, h_), F32))
    xs = (jnp.moveaxis(g, 2, 0), jnp.moveaxis(m_loc, 2, 0),
          jnp.moveaxis(c_loc, 2, 0), jnp.moveaxis(n_loc, 2, 0))
    _, (c_prev, n_prev, m_prev) = lax.scan(step, init, xs)
    c_prev = jnp.moveaxis(c_prev, 0, 2)
    n_prev = jnp.moveaxis(n_prev, 0, 2)
    m_prev = jnp.moveaxis(m_prev, 0, 2)

    causal = jnp.tril(jnp.ones((L, L), dtype=bool))
    dmat = bcum[..., :, None] - bcum[..., None, :] + ig[..., None, :]
    dmat = jnp.where(causal, dmat, -jnp.inf)
    m_inter = bcum + m_prev[..., None]
    m_t = jnp.maximum(m_inter, jnp.max(dmat, axis=-1))
    scores = jnp.einsum('bhctd,bhcsd->bhcts', qc, kc) * jnp.exp(dmat - m_t[..., None])
    inter_w = jnp.exp(m_inter - m_t)
    num = (jnp.einsum('bhcts,bhcse->bhcte', scores, vc)
           + inter_w[..., None] * jnp.einsum('bhctd,bhcde->bhcte', qc, c_prev))
    den = jnp.sum(scores, axis=-1) + inter_w * jnp.einsum('bhctd,bhcd->bhct', qc, n_prev)
    h = num / jnp.maximum(jnp.abs(den), jnp.exp(-m_t))[..., None]
    return h.reshape(b_, h_, s_, dv)


def moba_attention(q, k, v, slopes):
    b_, h_, s_, dh = q.shape
    nb = -(-s_ // MOBA_BLOCK)
    topk = min(MOBA_TOPK, nb)
    pad = nb * MOBA_BLOCK - s_
    kp = jnp.pad(k, ((0, 0), (0, 0), (0, pad), (0, 0)))
    vp = jnp.pad(v, ((0, 0), (0, 0), (0, pad), (0, 0)))
    kb = kp.reshape(b_, h_, nb, MOBA_BLOCK, dh)
    vb = vp.reshape(b_, h_, nb, MOBA_BLOCK, dh)
    k_mean = jnp.mean(kb.astype(F32), axis=3)
    scale = dh ** -0.5
    bi = jnp.arange(b_)[:, None, None, None]
    hi = jnp.arange(h_)[None, :, None, None]
    blk_pos = jnp.arange(MOBA_BLOCK)

    def one_chunk(c):
        start = c * MOBA_QCHUNK
        qs = lax.dynamic_slice_in_dim(q, start, MOBA_QCHUNK, axis=2).astype(F32)
        t = start + jnp.arange(MOBA_QCHUNK)
        j = start // MOBA_BLOCK
        gate = jnp.einsum('bhqd,bhnd->bhqn', qs, k_mean)
        gate = jnp.where(jnp.arange(nb) < j, gate, -jnp.inf)
        _, idx = lax.top_k(gate, topk)
        valid = jnp.arange(topk) < j
        ksel = kb[bi, hi, idx]
        vsel = vb[bi, hi, idx]
        pos_sel = idx[..., None] * MOBA_BLOCK + blk_pos
        s_sel = (jnp.einsum('bhqd,bhqrkd->bhqrk', qs, ksel) * scale
                 - slopes[:, None, None, None] * (t[:, None, None] - pos_sel).astype(F32))
        s_sel = jnp.where(valid[:, None], s_sel, -jnp.inf).reshape(b_, h_, MOBA_QCHUNK, topk * MOBA_BLOCK)
        kown = lax.dynamic_slice_in_dim(kp, j * MOBA_BLOCK, MOBA_BLOCK, axis=2)
        vown = lax.dynamic_slice_in_dim(vp, j * MOBA_BLOCK, MOBA_BLOCK, axis=2)
        dist_own = (t[:, None] - (j * MOBA_BLOCK + blk_pos)[None, :])
        s_own = (jnp.einsum('bhqd,bhkd->bhqk', qs, kown) * scale
                 - slopes[:, None, None] * dist_own.astype(F32))
        s_own = jnp.where(dist_own >= 0, s_own, -jnp.inf)
        p = jax.nn.softmax(jnp.concatenate([s_sel, s_own], axis=-1), axis=-1)
        p_sel = p[..., :topk * MOBA_BLOCK].reshape(b_, h_, MOBA_QCHUNK, topk, MOBA_BLOCK)
        return (jnp.einsum('bhqrk,bhqrke->bhqe', p_sel, vsel)
                + jnp.einsum('bhqk,bhke->bhqe', p[..., topk * MOBA_BLOCK:], vown))

    out = lax.map(one_chunk, jnp.arange(s_ // MOBA_QCHUNK))
    return jnp.moveaxis(out, 0, 2).reshape(b_, h_, s_, dh)


def diff_attention(q, k, v, lam, lam_init, g_sub, slopes):
    b_, h_, _, s_, d = q.shape
    scale = d ** -0.5
    key_pos = jnp.arange(s_)

    def one_block(c):
        start = c * Q_BLOCK
        qs = lax.dynamic_slice_in_dim(q, start, Q_BLOCK, axis=3).astype(F32)
        t = start + jnp.arange(Q_BLOCK)
        dist = t[:, None] - key_pos[None, :]
        s = (jnp.einsum('bhmqd,bhmkd->bhmqk', qs, k) * scale
             - slopes[:, None, None, None] * dist.astype(F32))
        s = jnp.where(dist >= 0, s, -jnp.inf)
        p = jax.nn.softmax(s, axis=-1)
        a = p[:, :, 0] - lam * p[:, :, 1]
        return jnp.einsum('bhqk,bhke->bhqe', a, v)

    out = lax.map(one_block, jnp.arange(s_ // Q_BLOCK))
    out = jnp.moveaxis(out, 0, 2).reshape(b_, h_, s_, v.shape[-1])
    return rmsnorm(out, g_sub) * (1.0 - lam_init)


def setup_inputs(seed: int = 0) -> dict:
    key = jax.random.key(seed)
    ks = jax.random.split(key, 24)
    nrm = lambda k, shape, scale: jax.random.normal(k, shape, F32) * scale
    gain = lambda k, n: 1.0 + 0.05 * jax.random.normal(k, (DEPTH, n), F32)
    return {
        "x": jax.random.normal(ks[0], (BATCH, SEQ, D_MODEL), F32),
        "ffn1_pre_norm": gain(ks[1], D_MODEL),
        "ffn1_w_gate": nrm(ks[2], (DEPTH, D_MODEL, D_FF), D_MODEL ** -0.5),
        "ffn1_w_up": nrm(ks[3], (DEPTH, D_MODEL, D_FF), D_MODEL ** -0.5),
        "ffn1_w_down": nrm(ks[4], (DEPTH, D_FF, D_MODEL), D_FF ** -0.5),
        "ffn1_post_norm": gain(ks[5], D_MODEL),
        "mix_pre_norm": gain(ks[6], D_MODEL),
        "w_in": nrm(ks[7], (DEPTH, D_MODEL, PROJ_WIDTH), D_MODEL ** -0.5),
        "conv_qk": nrm(ks[8], (DEPTH, CONV_W, 2 * M_QK_WIDTH), CONV_W ** -0.5),
        "igate_bias": nrm(ks[9], (DEPTH, M_HEADS), 0.1),
        "fgate_bias": jnp.linspace(3.0, 6.0, M_HEADS, dtype=F32)[None, :] + nrm(ks[10], (DEPTH, M_HEADS), 0.1),
        "lambda_q1": nrm(ks[11], (DEPTH, DF_DQK), 0.1),
        "lambda_k1": nrm(ks[12], (DEPTH, DF_DQK), 0.1),
        "lambda_q2": nrm(ks[13], (DEPTH, DF_DQK), 0.1),
        "lambda_k2": nrm(ks[14], (DEPTH, DF_DQK), 0.1),
        "diff_subln": gain(ks[15], DF_DV),
        "w_out": nrm(ks[16], (DEPTH, MIX_WIDTH, D_MODEL), MIX_WIDTH ** -0.5),
        "mix_post_norm": gain(ks[17], D_MODEL),
        "ffn2_pre_norm": gain(ks[18], D_MODEL),
        "ffn2_w_gate": nrm(ks[19], (DEPTH, D_MODEL, D_FF), D_MODEL ** -0.5),
        "ffn2_w_up": nrm(ks[20], (DEPTH, D_MODEL, D_FF), D_MODEL ** -0.5),
        "ffn2_w_down": nrm(ks[21], (DEPTH, D_FF, D_MODEL), D_FF ** -0.5),
        "ffn2_post_norm": gain(ks[22], D_MODEL),
    }


def reference(x, ffn1_pre_norm, ffn1_w_gate, ffn1_w_up, ffn1_w_down, ffn1_post_norm,
              mix_pre_norm, w_in, conv_qk, igate_bias, fgate_bias,
              lambda_q1, lambda_k1, lambda_q2, lambda_k2, diff_subln, w_out, mix_post_norm,
              ffn2_pre_norm, ffn2_w_gate, ffn2_w_up, ffn2_w_down, ffn2_post_norm):
    slopes = alibi_slopes()
    moba_slopes = slopes[0::2]
    diff_slopes = slopes[1::2]
    for l in range(DEPTH):
        h = rmsnorm(x, ffn1_pre_norm[l])
        x = x + 0.5 * rmsnorm(swiglu(h, ffn1_w_gate[l], ffn1_w_up[l], ffn1_w_down[l]), ffn1_post_norm[l])

        h = rmsnorm(x, mix_pre_norm[l])
        (mq, mk, mv, mo, mi, mf, aq, ak, av, dq, dk, dv) = split_cols(h @ w_in[l])
        b_, s_, _ = h.shape

        qk = jax.nn.silu(causal_conv(jnp.concatenate([mq, mk], axis=-1), conv_qk[l]))
        i_pre = (mi.astype(F32) + igate_bias[l].astype(F32)).transpose(0, 2, 1)
        f_pre = (mf.astype(F32) + fgate_bias[l].astype(F32)).transpose(0, 2, 1)
        h_m = mlstm_chunkwise(to_heads(qk[..., :M_QK_WIDTH], M_HEADS), to_heads(qk[..., M_QK_WIDTH:], M_HEADS),
                              to_heads(mv, M_HEADS), i_pre, f_pre)
        y_m = from_heads(h_m) * jax.nn.sigmoid(mo.astype(F32))

        y_a = from_heads(moba_attention(to_heads(aq, A_HEADS), to_heads(ak, A_HEADS),
                                        to_heads(av, A_HEADS), moba_slopes))

        dq5 = dq.reshape(b_, s_, DF_HEADS, 2, DF_DQK).transpose(0, 2, 3, 1, 4)
        dk5 = dk.reshape(b_, s_, DF_HEADS, 2, DF_DQK).transpose(0, 2, 3, 1, 4)
        lam_init = 0.8 - 0.6 * math.exp(-0.3 * l)
        lam = (jnp.exp(jnp.sum(lambda_q1[l].astype(F32) * lambda_k1[l].astype(F32)))
               - jnp.exp(jnp.sum(lambda_q2[l].astype(F32) * lambda_k2[l].astype(F32))) + lam_init)
        y_d = from_heads(diff_attention(dq5, dk5, to_heads(dv, DF_HEADS), lam, lam_init,
                                        diff_subln[l], diff_slopes))

        mix = jnp.concatenate([y_m, y_a, y_d], axis=-1).astype(x.dtype)
        x = x + rmsnorm(mix @ w_out[l], mix_post_norm[l])

        h = rmsnorm(x, ffn2_pre_norm[l])
        x = x + 0.5 * rmsnorm(swiglu(h, ffn2_w_gate[l], ffn2_w_up[l], ffn2_w_down[l]), ffn2_post_norm[l])
    return x
```

```python
import functools
import math

import jax
import jax.numpy as jnp
from jax import lax
from jax.experimental import pallas as pl
from jax.experimental.pallas import tpu as pltpu

F32 = jnp.float32
BF16 = jnp.bfloat16

D_MODEL = 1024
D_FF = 2816
RMS_EPS = 1e-6
N_HEADS = 4
HEAD_DIM = 64
LANES = 128
CONV_W = 4
MOBA_BLOCK = 256
MOBA_TOPK = 3
N_SOFTMAX_HEADS = 8
NEG = -1e30
VMEM_LIMIT = 56 * 1024 * 1024

COL_MQ, COL_MK, COL_MV, COL_MO = 0, 2, 4, 6
COL_AQ, COL_AK, COL_AV = 8, 10, 12
COL_DQ, COL_DK, COL_DV = 14, 18, 22
PROJ_COLS = 26 * LANES
QK_CONV_COLS = 4 * LANES

_SPLIT = (256, 256, 256, 256, 4, 4, 256, 256, 256, 512, 512, 512)
_OFF = [sum(_SPLIT[:i]) for i in range(len(_SPLIT))]


def _nt_dot(a, b, precision=None):
    return lax.dot_general(a, b, (((1,), (1,)), ((), ())),
                           preferred_element_type=F32, precision=precision)


def _rms_scale(x):
    return lax.rsqrt(jnp.mean(x * x, axis=-1, keepdims=True) + RMS_EPS)


def _ffn_kernel(x_ref, g1_ref, wg_ref, wu_ref, wd_ref, g2_ref, o_ref, h_ref, acc_ref):
    j = pl.program_id(1)

    @pl.when(j == 0)
    def _():
        x = x_ref[...]
        h_ref[...] = (x * _rms_scale(x) * g1_ref[...]).astype(BF16)
        acc_ref[...] = jnp.zeros_like(acc_ref)

    h = h_ref[...]
    g = jnp.dot(h, wg_ref[...], preferred_element_type=F32)
    u = jnp.dot(h, wu_ref[...], preferred_element_type=F32)
    a = (g * jax.nn.sigmoid(g) * u).astype(BF16)
    acc_ref[...] += jnp.dot(a, wd_ref[...], preferred_element_type=F32)

    @pl.when(j == pl.num_programs(1) - 1)
    def _():
        y = acc_ref[...]
        o_ref[...] = x_ref[...] + 0.5 * (y * _rms_scale(y) * g2_ref[...])


def _ffn(x, g1, wg, wu, wd, g2, *, tm, tf):
    t = x.shape[0]
    return pl.pallas_call(
        _ffn_kernel,
        grid=(t // tm, D_FF // tf),
        in_specs=[
            pl.BlockSpec((tm, D_MODEL), lambda i, j: (i, 0)),
            pl.BlockSpec((1, D_MODEL), lambda i, j: (0, 0)),
            pl.BlockSpec((D_MODEL, tf), lambda i, j: (0, j)),
            pl.BlockSpec((D_MODEL, tf), lambda i, j: (0, j)),
            pl.BlockSpec((tf, D_MODEL), lambda i, j: (j, 0)),
            pl.BlockSpec((1, D_MODEL), lambda i, j: (0, 0)),
        ],
        out_specs=pl.BlockSpec((tm, D_MODEL), lambda i, j: (i, 0)),
        out_shape=jax.ShapeDtypeStruct((t, D_MODEL), F32),
        scratch_shapes=[pltpu.VMEM((tm, D_MODEL), BF16), pltpu.VMEM((tm, D_MODEL), F32)],
        compiler_params=pltpu.CompilerParams(
            dimension_semantics=("arbitrary", "arbitrary"), vmem_limit_bytes=VMEM_LIMIT),
        name="ffn",
    )(x, g1, wg, wu, wd, g2)


def _inproj_kernel(x_ref, g_ref, w_ref, wgt_ref, gb_ref, conv_ref, o_ref, gt_ref, u_ref,
                   *, tm, tiles_per_seq, chunk):
    i = pl.program_id(0)
    x = x_ref[...]
    h = (x * _rms_scale(x) * g_ref[...]).astype(BF16)

    gt_ref[...] = _nt_dot(wgt_ref[...], h) + gb_ref[...]

    @pl.when(i % tiles_per_seq == 0)
    def _():
        u_ref[0:8, :] = jnp.zeros((8, QK_CONV_COLS), F32)

    u_ref[8:8 + tm, :] = jnp.dot(h, w_ref[:, 0:QK_CONV_COLS], preferred_element_type=F32)
    acc = conv_ref[CONV_W - 1:CONV_W, :] * u_ref[8:8 + tm, :]
    for tap in range(CONV_W - 1):
        acc = acc + conv_ref[tap:tap + 1, :] * u_ref[pl.ds(8 - (CONV_W - 1) + tap, tm), :]
    o_ref[:, 0:QK_CONV_COLS] = (acc * jax.nn.sigmoid(acc)).astype(BF16)
    u_ref[0:8, :] = u_ref[tm:tm + 8, :]

    for c in range(QK_CONV_COLS, PROJ_COLS, chunk):
        w = min(chunk, PROJ_COLS - c)
        o_ref[:, c:c + w] = jnp.dot(h, w_ref[:, c:c + w], preferred_element_type=F32).astype(BF16)


def _inproj(x, g, w, wgt, gb, conv, *, tm, seq):
    t = x.shape[0]
    kern = functools.partial(_inproj_kernel, tm=tm, tiles_per_seq=seq // tm, chunk=512)
    return pl.pallas_call(
        kern,
        grid=(t // tm,),
        in_specs=[
            pl.BlockSpec((tm, D_MODEL), lambda i: (i, 0)),
            pl.BlockSpec((1, D_MODEL), lambda i: (0, 0)),
            pl.BlockSpec((D_MODEL, PROJ_COLS), lambda i: (0, 0)),
            pl.BlockSpec((16, D_MODEL), lambda i: (0, 0)),
            pl.BlockSpec((16, 1), lambda i: (0, 0)),
            pl.BlockSpec((CONV_W, QK_CONV_COLS), lambda i: (0, 0)),
        ],
        out_specs=[
            pl.BlockSpec((tm, PROJ_COLS), lambda i: (i, 0)),
            pl.BlockSpec((16, tm), lambda i: (0, i)),
        ],
        out_shape=[
            jax.ShapeDtypeStruct((t, PROJ_COLS), BF16),
            jax.ShapeDtypeStruct((16, t), F32),
        ],
        scratch_shapes=[pltpu.VMEM((tm + 8, QK_CONV_COLS), F32)],
        compiler_params=pltpu.CompilerParams(
            dimension_semantics=("arbitrary",), vmem_limit_bytes=VMEM_LIMIT),
        name="inproj",
    )(x, g, w, wgt, gb, conv)


def _lane_scan(x, lane, op, identity):
    shift = 1
    while shift < LANES:
        rolled = pltpu.roll(x, shift, axis=1)
        x = op(x, jnp.where(lane >= shift, rolled, identity))
        shift *= 2
    return x


def _gateprep_kernel(gt_ref, a_ref, cm_ref, *, seq):
    lane = lax.broadcasted_iota(jnp.int32, (8, LANES), 1)
    sum_carry = jnp.zeros((8, 1), F32)
    max_carry = jnp.full((8, 1), -jnp.inf, F32)
    for blk in range(seq // LANES):
        sl = slice(blk * LANES, (blk + 1) * LANES)
        ig = gt_ref[0:8, sl]
        fp = gt_ref[8:16, sl]
        logf = jnp.minimum(fp, 0.0) - jnp.log1p(jnp.exp(-jnp.abs(fp)))
        bcum = _lane_scan(logf, lane, jnp.add, 0.0) + sum_carry
        a = ig - bcum
        ca = jnp.maximum(_lane_scan(a, lane, jnp.maximum, -jnp.inf), max_carry)
        a_ref[0, :, sl] = a
        cm_ref[0, 0:8, sl] = ca
        cm_ref[0, 8:16, sl] = bcum + ca
        sum_carry = bcum[:, LANES - 1:LANES]
        max_carry = ca[:, LANES - 1:LANES]


def _gateprep(gt, *, batch, seq):
    kern = functools.partial(_gateprep_kernel, seq=seq)
    return pl.pallas_call(
        kern,
        grid=(batch,),
        in_specs=[pl.BlockSpec((16, seq), lambda b: (0, b))],
        out_specs=[
            pl.BlockSpec((1, 8, seq), lambda b: (b, 0, 0)),
            pl.BlockSpec((1, 16, seq), lambda b: (b, 0, 0)),
        ],
        out_shape=[
            jax.ShapeDtypeStruct((batch, 8, seq), F32),
            jax.ShapeDtypeStruct((batch, 16, seq), F32),
        ],
        compiler_params=pltpu.CompilerParams(dimension_semantics=("arbitrary",)),
        name="gateprep",
    )(gt)


def _split_heads(q):
    lane = lax.broadcasted_iota(jnp.int32, q.shape, 1)
    zero = jnp.zeros_like(q)
    return jnp.where(lane < HEAD_DIM, q, zero), jnp.where(lane >= HEAD_DIM, q, zero)


def _rel_pos(tq, tk):
    r = lax.broadcasted_iota(jnp.int32, (tq, tk), 0)
    c = lax.broadcasted_iota(jnp.int32, (tq, tk), 1)
    return c - r


def _softmax_step(s, vb, m, l, acc):
    m_new = jnp.maximum(m, jnp.max(s, axis=1, keepdims=True))
    alpha = jnp.exp(m - m_new)
    p = jnp.exp(s - m_new)
    l_new = alpha * l + jnp.sum(p, axis=1, keepdims=True)
    acc_new = alpha * acc + jnp.dot(p.astype(BF16), vb, preferred_element_type=F32)
    return m_new, l_new, acc_new


def _mlstm_kernel(q_ref, k_ref, v_ref, mo_ref, a_ref, col_ref, o_ref, *, tq):
    qi = pl.program_id(2)
    q = q_ref[0] * (HEAD_DIM ** -0.5)
    q_heads = _split_heads(q.astype(BF16))
    cols = col_ref[0, 0]
    causal = _rel_pos(tq, tq) <= 0

    def block(j, carry, diag):
        start = pl.multiple_of(j * tq, tq)
        kb = k_ref[0, pl.ds(start, tq), :]
        vb = v_ref[0, pl.ds(start, tq), :]
        a_rows = a_ref[0, 0, j]
        out = []
        for hd in range(2):
            num, den = carry[hd]
            s = _nt_dot(q_heads[hd], kb)
            e = a_rows[hd:hd + 1, :] - cols[:, hd:hd + 1]
            if diag:
                e = jnp.where(causal, e, NEG)
            sc = s * jnp.exp(e)
            den = den + jnp.sum(sc, axis=1, keepdims=True)
            num = num + jnp.dot(sc.astype(BF16), vb, preferred_element_type=F32)
            out.append((num, den))
        return tuple(out)

    zero = (jnp.zeros((tq, LANES), F32), jnp.zeros((tq, 1), F32))
    carry = lax.fori_loop(0, qi, lambda j, c: block(j, c, False), (zero, zero))
    (num_a, den_a), (num_b, den_b) = block(qi, carry, True)

    h_a = num_a / jnp.maximum(jnp.abs(den_a), jnp.exp(-cols[:, 2:3]))
    h_b = num_b / jnp.maximum(jnp.abs(den_b), jnp.exp(-cols[:, 3:4]))
    lane = lax.broadcasted_iota(jnp.int32, (tq, LANES), 1)
    h = jnp.where(lane < HEAD_DIM, h_a, h_b)
    o_ref[0] = (h * jax.nn.sigmoid(mo_ref[0].astype(F32))).astype(BF16)


def _mlstm(proj, a_rows, cols, *, tq):
    batch, seq, _ = proj.shape
    kern = functools.partial(_mlstm_kernel, tq=tq)
    nb = seq // tq
    return pl.pallas_call(
        kern,
        grid=(batch, 2, nb),
        in_specs=[
            pl.BlockSpec((1, tq, LANES), lambda b, p, i: (b, i, COL_MQ + p)),
            pl.BlockSpec((1, seq, LANES), lambda b, p, i: (b, 0, COL_MK + p)),
            pl.BlockSpec((1, seq, LANES), lambda b, p, i: (b, 0, COL_MV + p)),
            pl.BlockSpec((1, tq, LANES), lambda b, p, i: (b, i, COL_MO + p)),
            pl.BlockSpec((1, 1, nb, 2, tq), lambda b, p, i: (b, p, 0, 0, 0)),
            pl.BlockSpec((1, 1, tq, 4), lambda b, p, i: (b, p, i, 0)),
        ],
        out_specs=pl.BlockSpec((1, tq, LANES), lambda b, p, i: (b, i, p)),
        out_shape=jax.ShapeDtypeStruct((batch, seq, 2 * LANES), BF16),
        compiler_params=pltpu.CompilerParams(
            dimension_semantics=("arbitrary", "arbitrary", "arbitrary"),
            vmem_limit_bytes=VMEM_LIMIT),
        name="mlstm",
    )(proj, proj, proj, proj, a_rows, cols)


def _moba_kernel(slope_ref, q_ref, k_ref, v_ref, o_ref, kmean_ref, m_ref, l_ref, acc_ref, *, seq):
    tq = MOBA_BLOCK
    nb = seq // tq
    p = pl.program_id(1)
    j = pl.program_id(2)

    @pl.when(j == 0)
    def _():
        for n in range(nb):
            kb = k_ref[0, n * tq:(n + 1) * tq, :].astype(F32)
            kmean_ref[n:n + 1, :] = jnp.sum(kb, axis=0, keepdims=True) * (1.0 / tq)

    q = q_ref[0]
    gate_q = _split_heads(q.astype(F32))
    score_q = _split_heads(q * (HEAD_DIM ** -0.5))
    rel = _rel_pos(tq, tq)
    relf = rel.astype(F32)

    nidx = lax.broadcasted_iota(jnp.int32, (nb, tq), 0)
    sel_rows = []
    for hd in range(2):
        g = _nt_dot(kmean_ref[...], gate_q[hd], precision=lax.Precision.HIGHEST)
        g = jnp.where(nidx < j, g, -jnp.inf)
        cnt = jnp.zeros((nb, tq), F32)
        for mblk in range(nb):
            gm = g[mblk:mblk + 1, :]
            beats = jnp.where(gm > g, 1.0, jnp.where((gm == g) & (nidx > mblk), 1.0, 0.0))
            cnt = cnt + beats
        sel_rows.append(jnp.where(nidx < j, jnp.where(cnt < MOBA_TOPK, 1.0, 0.0), 0.0))
    sel_t = jnp.concatenate(sel_rows, axis=0).astype(BF16)
    eye = jnp.where(rel == 0, 1.0, 0.0).astype(BF16)
    sel = _nt_dot(eye, sel_t)

    slopes = (slope_ref[2 * p], slope_ref[2 * p + 1])

    kb = k_ref[0, pl.ds(pl.multiple_of(j * tq, tq), tq), :]
    vb = v_ref[0, pl.ds(pl.multiple_of(j * tq, tq), tq), :]
    for hd in range(2):
        s = _nt_dot(score_q[hd], kb) + slopes[hd] * relf
        s = jnp.where(rel <= 0, s, NEG)
        m0 = jnp.full((tq, 1), NEG, F32)
        m, l, acc = _softmax_step(s, vb, m0, jnp.zeros((tq, 1), F32), jnp.zeros((tq, LANES), F32))
        m_ref[hd] = m
        l_ref[hd] = l
        acc_ref[hd] = acc

    for n in range(nb - 1):
        @pl.when(n < j)
        def _(n=n):
            kb = k_ref[0, n * tq:(n + 1) * tq, :]
            vb = v_ref[0, n * tq:(n + 1) * tq, :]
            shift = ((n - j) * tq).astype(F32)
            for hd in range(2):
                s = _nt_dot(score_q[hd], kb) + slopes[hd] * (relf + shift)
                keep = sel[:, hd * nb + n:hd * nb + n + 1] > 0.5
                s = jnp.where(keep, s, NEG)
                m, l, acc = _softmax_step(s, vb, m_ref[hd], l_ref[hd], acc_ref[hd])
                m_ref[hd] = m
                l_ref[hd] = l
                acc_ref[hd] = acc

    lane = lax.broadcasted_iota(jnp.int32, (tq, LANES), 1)
    o = jnp.where(lane < HEAD_DIM, acc_ref[0] / l_ref[0], acc_ref[1] / l_ref[1])
    o_ref[0] = o.astype(BF16)


def _moba(proj, slopes):
    batch, seq, _ = proj.shape
    tq = MOBA_BLOCK
    nb = seq // tq
    kern = functools.partial(_moba_kernel, seq=seq)
    return pl.pallas_call(
        kern,
        grid=(batch, 2, nb),
        in_specs=[
            pl.BlockSpec(memory_space=pltpu.SMEM),
            pl.BlockSpec((1, tq, LANES), lambda b, p, i: (b, i, COL_AQ + p)),
            pl.BlockSpec((1, seq, LANES), lambda b, p, i: (b, 0, COL_AK + p)),
            pl.BlockSpec((1, seq, LANES), lambda b, p, i: (b, 0, COL_AV + p)),
        ],
        out_specs=pl.BlockSpec((1, tq, LANES), lambda b, p, i: (b, i, p)),
        out_shape=jax.ShapeDtypeStruct((batch, seq, 2 * LANES), BF16),
        scratch_shapes=[
            pltpu.VMEM((nb, LANES), F32),
            pltpu.VMEM((2, tq, 1), F32),
            pltpu.VMEM((2, tq, 1), F32),
            pltpu.VMEM((2, tq, LANES), F32),
        ],
        compiler_params=pltpu.CompilerParams(
            dimension_semantics=("arbitrary", "arbitrary", "arbitrary"),
            vmem_limit_bytes=VMEM_LIMIT),
        name="moba",
    )(slopes, proj, proj, proj)


def _diff_kernel(slope_ref, lam_ref, q_ref, k_ref, v_ref, g_ref, o_ref, *, tq, lam_init):
    hd = pl.program_id(1)
    qi = pl.program_id(2)
    slope = slope_ref[hd]
    q_maps = _split_heads(q_ref[0] * (HEAD_DIM ** -0.5))
    rel = _rel_pos(tq, tq)
    relf = rel.astype(F32)

    def block(j, carry, diag):
        start = pl.multiple_of(j * tq, tq)
        kb = k_ref[0, pl.ds(start, tq), :]
        vb = v_ref[0, pl.ds(start, tq), :]
        bias = slope * (relf + ((j - qi) * tq).astype(F32))
        out = []
        for mp in range(2):
            s = _nt_dot(q_maps[mp], kb) + bias
            if diag:
                s = jnp.where(rel <= 0, s, NEG)
            out.append(_softmax_step(s, vb, *carry[mp]))
        return tuple(out)

    init = (jnp.full((tq, 1), NEG, F32), jnp.zeros((tq, 1), F32), jnp.zeros((tq, LANES), F32))
    carry = lax.fori_loop(0, qi, lambda j, c: block(j, c, False), (init, init))
    (_, l1, acc1), (_, l2, acc2) = block(qi, carry, True)

    lp = lam_ref[...]
    lam = (jnp.exp(jnp.sum(lp[0:1, :] * lp[1:2, :], axis=1, keepdims=True))
           - jnp.exp(jnp.sum(lp[2:3, :] * lp[3:4, :], axis=1, keepdims=True)) + lam_init)
    o = acc1 / l1 - lam * (acc2 / l2)
    o_ref[0] = (o * _rms_scale(o) * g_ref[...] * (1.0 - lam_init)).astype(BF16)


def _diffattn(proj, slopes, lam_params, g_sub, *, tq, lam_init):
    batch, seq, _ = proj.shape
    kern = functools.partial(_diff_kernel, tq=tq, lam_init=lam_init)
    return pl.pallas_call(
        kern,
        grid=(batch, N_HEADS, seq // tq),
        in_specs=[
            pl.BlockSpec(memory_space=pltpu.SMEM),
            pl.BlockSpec((8, LANES), lambda b, h, i: (0, 0)),
            pl.BlockSpec((1, tq, LANES), lambda b, h, i: (b, i, COL_DQ + h)),
            pl.BlockSpec((1, seq, LANES), lambda b, h, i: (b, 0, COL_DK + h)),
            pl.BlockSpec((1, seq, LANES), lambda b, h, i: (b, 0, COL_DV + h)),
            pl.BlockSpec((1, LANES), lambda b, h, i: (0, 0)),
        ],
        out_specs=pl.BlockSpec((1, tq, LANES), lambda b, h, i: (b, i, h)),
        out_shape=jax.ShapeDtypeStruct((batch, seq, N_HEADS * LANES), BF16),
        compiler_params=pltpu.CompilerParams(
            dimension_semantics=("arbitrary", "arbitrary", "arbitrary"),
            vmem_limit_bytes=VMEM_LIMIT),
        name="diffattn",
    )(slopes, lam_params, proj, proj, proj, g_sub)


def _outproj_kernel(x_ref, ym_ref, ya_ref, yd_ref, w_ref, g_ref, o_ref):
    y = jnp.dot(ym_ref[...], w_ref[0:256, :], preferred_element_type=F32)
    y = y + jnp.dot(ya_ref[...], w_ref[256:512, :], preferred_element_type=F32)
    y = y + jnp.dot(yd_ref[...], w_ref[512:1024, :], preferred_element_type=F32)
    o_ref[...] = x_ref[...] + y * _rms_scale(y) * g_ref[...]


def _outproj(x, ym, ya, yd, w, g, *, tm):
    t = x.shape[0]
    return pl.pallas_call(
        _outproj_kernel,
        grid=(t // tm,),
        in_specs=[
            pl.BlockSpec((tm, D_MODEL), lambda i: (i, 0)),
            pl.BlockSpec((tm, 256), lambda i: (i, 0)),
            pl.BlockSpec((tm, 256), lambda i: (i, 0)),
            pl.BlockSpec((tm, 512), lambda i: (i, 0)),
            pl.BlockSpec((D_MODEL, D_MODEL), lambda i: (0, 0)),
            pl.BlockSpec((1, D_MODEL), lambda i: (0, 0)),
        ],
        out_specs=pl.BlockSpec((tm, D_MODEL), lambda i: (i, 0)),
        out_shape=jax.ShapeDtypeStruct((t, D_MODEL), F32),
        compiler_params=pltpu.CompilerParams(
            dimension_semantics=("arbitrary",), vmem_limit_bytes=VMEM_LIMIT),
        name="outproj",
    )(x, ym, ya, yd, w, g)


def _arrange_w_in(w):
    cols = [w[:, _OFF[i]:_OFF[i] + _SPLIT[i]] for i in range(len(_SPLIT))]
    mq, mk, mv, mo, mi, mf, aq, ak, av, dq, dk, dv = cols
    main = jnp.concatenate([mq, mk, mv, mo, aq, ak, av, dq, dk, dv], axis=1).astype(BF16)
    zeros = jnp.zeros((N_HEADS, w.shape[0]), w.dtype)
    gates_t = jnp.concatenate([mi.T, zeros, mf.T, zeros], axis=0).astype(BF16)
    return main, gates_t


def kernel(x, ffn1_pre_norm, ffn1_w_gate, ffn1_w_up, ffn1_w_down, ffn1_post_norm, mix_pre_norm, w_in, conv_qk, igate_bias, fgate_bias, lambda_q1, lambda_k1, lambda_q2, lambda_k2, diff_subln, w_out, mix_post_norm, ffn2_pre_norm, ffn2_w_gate, ffn2_w_up, ffn2_w_down, ffn2_post_norm):
    batch, seq, _ = x.shape
    t = batch * seq
    depth = w_in.shape[0]
    tm = 512
    tq = 256
    nb = seq // tq

    slopes = jnp.exp2(-8.0 * jnp.arange(1, N_SOFTMAX_HEADS + 1, dtype=F32) / N_SOFTMAX_HEADS)
    moba_slopes = slopes[0::2]
    diff_slopes = slopes[1::2]

    row = lambda v: v.reshape(1, -1).astype(F32)
    xt = x.reshape(t, D_MODEL)
    for l in range(depth):
        xt = _ffn(xt, row(ffn1_pre_norm[l]), ffn1_w_gate[l].astype(BF16), ffn1_w_up[l].astype(BF16),
                  ffn1_w_down[l].astype(BF16), row(ffn1_post_norm[l]), tm=tm, tf=256)

        w_main, w_gates_t = _arrange_w_in(w_in[l])
        zeros4 = jnp.zeros((N_HEADS,), F32)
        gate_bias = jnp.concatenate([igate_bias[l].astype(F32), zeros4,
                                     fgate_bias[l].astype(F32), zeros4]).reshape(16, 1)
        proj, gates_t = _inproj(xt, row(mix_pre_norm[l]), w_main, w_gates_t, gate_bias,
                                conv_qk[l].astype(F32), tm=tm, seq=seq)
        proj = proj.reshape(batch, seq, PROJ_COLS)

        a_rows, cm = _gateprep(gates_t, batch=batch, seq=seq)
        a_rows = a_rows[:, :N_HEADS].reshape(batch, 2, 2, nb, tq).transpose(0, 1, 3, 2, 4)
        ca = cm[:, 0:N_HEADS].reshape(batch, 2, 2, seq)
        mf = cm[:, 8:8 + N_HEADS].reshape(batch, 2, 2, seq)
        cols = jnp.concatenate([ca, mf], axis=2).transpose(0, 1, 3, 2)

        y_m = _mlstm(proj, a_rows, cols, tq=tq)
        y_a = _moba(proj, moba_slopes)

        lam_init = 0.8 - 0.6 * math.exp(-0.3 * l)
        lam_rows = jnp.stack([lambda_q1[l], lambda_k1[l], lambda_q2[l], lambda_k2[l]]).astype(F32)
        lam_params = jnp.zeros((8, LANES), F32).at[0:4, 0:HEAD_DIM].set(lam_rows)
        y_d = _diffattn(proj, diff_slopes, lam_params, row(diff_subln[l]), tq=tq, lam_init=lam_init)

        xt = _outproj(xt, y_m.reshape(t, -1), y_a.reshape(t, -1), y_d.reshape(t, -1),
                      w_out[l].astype(BF16), row(mix_post_norm[l]), tm=tm)

        xt = _ffn(xt, row(ffn2_pre_norm[l]), ffn2_w_gate[l].astype(BF16), ffn2_w_up[l].astype(BF16),
                  ffn2_w_down[l].astype(BF16), row(ffn2_post_norm[l]), tm=tm, tf=256)
    return xt.reshape(batch, seq, D_MODEL)
```
